```python
import jax, jax.numpy as jnp
from jax import lax
import numpy as np

D_MODEL = 2048
BATCH = 4
SEQ = 4096
DEPTH = 1
DEC_BATCH = 128
DEC_SEQ = 4
PAST_LEN = 16384
PAGE_SIZE = 128

RET_HEADS = 8
RET_DK = 64
RET_DV = 128
RET_WIDTH = RET_HEADS * RET_DV
RET_CHUNK = 128
SWA_HEADS = 16
SWA_KV_HEADS = 2
SWA_DH = 64
SWA_GROUP = SWA_HEADS // SWA_KV_HEADS
SWA_WIDTH = SWA_HEADS * SWA_DH
WINDOW = 128
SWA_BLOCK = 128
MIX_WIDTH = RET_WIDTH + SWA_WIDTH
PROJ_SIZES = (RET_HEADS * RET_DK, RET_HEADS * RET_DK, RET_WIDTH, RET_WIDTH,
              SWA_WIDTH, SWA_KV_HEADS * SWA_DH, SWA_KV_HEADS * SWA_DH)
PROJ_WIDTH = 2 * RET_HEADS * RET_DK + 2 * RET_WIDTH + SWA_WIDTH + 2 * SWA_KV_HEADS * SWA_DH
D_FF = 5632
CONV_W = 3
ROPE_THETA = 10000.0
EPS = 1e-6

kernel_name = 'hymba_retention_swa_sink_convffn_step'

F32 = jnp.float32


def rmsnorm(x, w):
    xf = x.astype(F32)
    y = xf * lax.rsqrt(jnp.mean(xf * xf, axis=-1, keepdims=True) + EPS) * w.astype(F32)
    return y.astype(x.dtype)


def rope(x, pos):
    half = x.shape[-1] // 2
    inv = ROPE_THETA ** (-jnp.arange(half, dtype=F32) / half)
    ang = pos.astype(F32)[:, None] * inv[None, :]
    cos = jnp.cos(ang)[None, :, None, :]
    sin = jnp.sin(ang)[None, :, None, :]
    x1 = x[..., :half].astype(F32)
    x2 = x[..., half:].astype(F32)
    return jnp.concatenate([x1 * cos - x2 * sin, x2 * cos + x1 * sin], axis=-1).astype(x.dtype)


def ret_log_decay():
    return jnp.log1p(-jnp.exp2(-5.0 - jnp.arange(RET_HEADS, dtype=F32)))


def retention_chunk(S, qkv):
    q, k, v = qkv
    C = q.shape[1]
    lg = ret_log_decay()
    idx = jnp.arange(C, dtype=F32)
    diff = idx[:, None] - idx[None, :]
    causal = diff >= 0
    dmask = jnp.where(causal[None], jnp.exp(jnp.where(causal, diff, 0.0)[None] * lg[:, None, None]), 0.0)
    scores = jnp.einsum('bihd,bjhd->bhij', q, k) * dmask[None]
    o = jnp.einsum('bhij,bjhv->bihv', scores, v)
    o = o + jnp.einsum('bihd,bhdv->bihv', q, S) * jnp.exp((idx + 1.0)[:, None] * lg[None, :])[None, :, :, None]
    k_dec = k * jnp.exp((C - 1.0 - idx)[:, None] * lg[None, :])[None, :, :, None]
    S_new = S * jnp.exp(C * lg)[None, :, None, None] + jnp.einsum('bjhd,bjhv->bhdv', k_dec, v)
    return S_new, o


def retention_prompt(q, k, v):
    B, T = q.shape[:2]
    nc = T // RET_CHUNK
    def chunks(x):
        return jnp.moveaxis(x.astype(F32).reshape(B, nc, RET_CHUNK, x.shape[2], x.shape[3]), 1, 0)
    S0 = jnp.zeros((B, RET_HEADS, RET_DK, RET_DV), F32)
    S_fin, o = lax.scan(retention_chunk, S0, (chunks(q), chunks(k), chunks(v)))
    o = jnp.moveaxis(o, 0, 1).reshape(B, T, RET_HEADS, RET_DV)
    return S_fin, o


def sink_attention(q, kk, vv, qpos, kpos, sinks):
    s = jnp.einsum('bnqkgd,bnskd->bnkgqs', q.astype(F32), kk.astype(F32)) * (SWA_DH ** -0.5)
    rel = qpos[:, :, None] - kpos[:, None, :]
    mask = (rel >= 0) & (rel <= WINDOW) & (kpos[:, None, :] >= 0)
    s = jnp.where(mask[None, :, None, None], s, -jnp.inf)
    sink = sinks.astype(F32).reshape(1, 1, SWA_KV_HEADS, SWA_GROUP, 1, 1)
    m = jnp.maximum(jnp.max(s, axis=-1, keepdims=True), sink)
    p = jnp.exp(s - m)
    p = p / (jnp.sum(p, axis=-1, keepdims=True) + jnp.exp(sink - m))
    return jnp.einsum('bnkgqs,bnskd->bnqkgd', p, vv.astype(F32))


def swa_prompt(q, k, v, sinks):
    B, T = q.shape[:2]
    nb = T // SWA_BLOCK
    qb = q.reshape(B, nb, SWA_BLOCK, SWA_KV_HEADS, SWA_GROUP, SWA_DH)
    def band(x):
        xb = x.reshape(B, nb, SWA_BLOCK, SWA_KV_HEADS, SWA_DH)
        prev = jnp.concatenate([jnp.zeros_like(xb[:, :1]), xb[:, :-1]], axis=1)
        return jnp.concatenate([prev, xb], axis=2)
    qpos = jnp.arange(T, dtype=jnp.int32).reshape(nb, SWA_BLOCK)
    kpos = qpos[:, :1] - SWA_BLOCK + jnp.arange(2 * SWA_BLOCK, dtype=jnp.int32)[None, :]
    o = sink_attention(qb, band(k), band(v), qpos, kpos, sinks)
    return o.reshape(B, T, SWA_WIDTH)


def swa_sample(q, k, v, kbuf, vbuf, pos, sinks):
    B, T = q.shape[:2]
    kk = jnp.concatenate([kbuf.astype(k.dtype), k], axis=1)
    vv = jnp.concatenate([vbuf.astype(v.dtype), v], axis=1)
    qb = q.reshape(B, 1, T, SWA_KV_HEADS, SWA_GROUP, SWA_DH)
    kpos = jnp.concatenate([PAST_LEN - WINDOW + jnp.arange(WINDOW, dtype=jnp.int32), pos])[None, :]
    o = sink_attention(qb, kk[:, None], vv[:, None], pos[None, :], kpos, sinks)
    return o.reshape(B, T, SWA_WIDTH), kk[:, -WINDOW:], vv[:, -WINDOW:]


def mixer_inputs(h, pos, w_in, q_norm_w, k_norm_w):
    B, T, _ = h.shape
    proj = jnp.einsum('btd,de->bte', h, w_in)
    split_at = np.cumsum(PROJ_SIZES)[:-1].tolist()
    rq, rk, rv, rg, sq, sk, sv = jnp.split(proj, split_at, axis=-1)
    rq = rope(rq.reshape(B, T, RET_HEADS, RET_DK), pos)
    rk = rope(rk.reshape(B, T, RET_HEADS, RET_DK), pos) * (RET_DK ** -0.5)
    rv = rv.reshape(B, T, RET_HEADS, RET_DV)
    sq = rope(rmsnorm(sq.reshape(B, T, SWA_HEADS, SWA_DH), q_norm_w), pos)
    sk = rope(rmsnorm(sk.reshape(B, T, SWA_KV_HEADS, SWA_DH), k_norm_w), pos)
    sv = sv.reshape(B, T, SWA_KV_HEADS, SWA_DH)
    return rq, rk, rv, rg, sq, sk, sv


def mixer_output(ret_o, rg, swa_o, ret_gn_w, w_o, dtype):
    B, T = ret_o.shape[:2]
    mu = jnp.mean(ret_o, axis=-1, keepdims=True)
    var = jnp.mean(jnp.square(ret_o - mu), axis=-1, keepdims=True)
    on = ((ret_o - mu) * lax.rsqrt(var + EPS)).reshape(B, T, RET_WIDTH) * ret_gn_w.astype(F32)
    ret_out = jax.nn.silu(rg.astype(F32)) * on
    cat = jnp.concatenate([ret_out, swa_o], axis=-1).astype(dtype)
    return jnp.einsum('bte,ed->btd', cat, w_o)


def conv_ffn(h, conv_buf, ffn_norm_w, w_up, conv_w, conv_b, w_down):
    T = h.shape[1]
    hn = rmsnorm(h, ffn_norm_w)
    up = jnp.einsum('btd,df->btf', hn, w_up)
    g, u = jnp.split(up, 2, axis=-1)
    gp = jnp.concatenate([conv_buf.astype(g.dtype), g], axis=1)
    gc = conv_b + sum(gp[:, j:j + T] * conv_w[j] for j in range(CONV_W))
    y = jnp.einsum('btf,fd->btd', jax.nn.silu(gc) * u, w_down)
    return y, gp[:, -(CONV_W - 1):]


def setup_inputs(seed: int = 0) -> dict:
    key = jax.random.key(seed)
    ks = jax.random.split(key, 18)
    def nrm(k, shape, scale):
        return jax.random.normal(k, shape, F32) * scale
    return {
        'x_prompt': nrm(ks[0], (BATCH, SEQ, D_MODEL), 1.0),
        'x_sample': nrm(ks[1], (DEC_BATCH, DEC_SEQ, D_MODEL), 1.0),
        'state_ret': nrm(ks[2], (DEPTH, DEC_BATCH, RET_HEADS, RET_DK, RET_DV), 0.5),
        'cache_swa_k': nrm(ks[3], (DEPTH, DEC_BATCH, WINDOW, SWA_KV_HEADS, SWA_DH), 1.0),
        'cache_swa_v': nrm(ks[4], (DEPTH, DEC_BATCH, WINDOW, SWA_KV_HEADS, SWA_DH), 1.0),
        'state_conv': nrm(ks[5], (DEPTH, DEC_BATCH, CONV_W - 1, D_FF), 1.0),
        'attn_norm_w': 1.0 + nrm(ks[6], (DEPTH, D_MODEL), 0.02),
        'w_in': nrm(ks[7], (DEPTH, D_MODEL, PROJ_WIDTH), D_MODEL ** -0.5),
        'swa_q_norm_w': 1.0 + nrm(ks[8], (DEPTH, SWA_DH), 0.02),
        'swa_k_norm_w': 1.0 + nrm(ks[9], (DEPTH, SWA_DH), 0.02),
        'swa_sinks': nrm(ks[10], (DEPTH, SWA_HEADS), 1.0),
        'ret_gn_w': 1.0 + nrm(ks[11], (DEPTH, RET_WIDTH), 0.02),
        'w_o': nrm(ks[12], (DEPTH, MIX_WIDTH, D_MODEL), MIX_WIDTH ** -0.5),
        'ffn_norm_w': 1.0 + nrm(ks[13], (DEPTH, D_MODEL), 0.02),
        'w_up': nrm(ks[14], (DEPTH, D_MODEL, 2 * D_FF), D_MODEL ** -0.5),
        'conv_w': nrm(ks[15], (DEPTH, CONV_W, D_FF), CONV_W ** -0.5),
        'conv_b': nrm(ks[16], (DEPTH, D_FF), 0.01),
        'w_down': nrm(ks[17], (DEPTH, D_FF, D_MODEL), D_FF ** -0.5),
    }


def reference(x_prompt, x_sample, state_ret, cache_swa_k, cache_swa_v, state_conv,
              attn_norm_w, w_in, swa_q_norm_w, swa_k_norm_w, swa_sinks, ret_gn_w, w_o,
              ffn_norm_w, w_up, conv_w, conv_b, w_down):
    pos_p = jnp.arange(SEQ, dtype=jnp.int32)
    pos_s = PAST_LEN + jnp.arange(DEC_SEQ, dtype=jnp.int32)
    hp, hs = x_prompt, x_sample
    ret_p, ret_s, kp, ks_, vp, vs, cp, cs = [], [], [], [], [], [], [], []
    for l in range(DEPTH):
        a = rmsnorm(hp, attn_norm_w[l])
        rq, rk, rv, rg, sq, sk, sv = mixer_inputs(a, pos_p, w_in[l], swa_q_norm_w[l], swa_k_norm_w[l])
        S_p, ro = retention_prompt(rq, rk, rv)
        so = swa_prompt(sq, sk, sv, swa_sinks[l])
        hp = hp + mixer_output(ro, rg, so, ret_gn_w[l], w_o[l], hp.dtype)
        zbuf = jnp.zeros((hp.shape[0], CONV_W - 1, D_FF), hp.dtype)
        f, cbuf_p = conv_ffn(hp, zbuf, ffn_norm_w[l], w_up[l], conv_w[l], conv_b[l], w_down[l])
        hp = hp + f
        ret_p.append(S_p.astype(state_ret.dtype))
        kp.append(sk[:, -WINDOW:].astype(cache_swa_k.dtype))
        vp.append(sv[:, -WINDOW:].astype(cache_swa_v.dtype))
        cp.append(cbuf_p.astype(state_conv.dtype))
        a = rmsnorm(hs, attn_norm_w[l])
        rq, rk, rv, rg, sq, sk, sv = mixer_inputs(a, pos_s, w_in[l], swa_q_norm_w[l], swa_k_norm_w[l])
        S_s, ro = retention_chunk(state_ret[l].astype(F32), (rq.astype(F32), rk.astype(F32), rv.astype(F32)))
        so, kbuf_s, vbuf_s = swa_sample(sq, sk, sv, cache_swa_k[l], cache_swa_v[l], pos_s, swa_sinks[l])
        hs = hs + mixer_output(ro, rg, so, ret_gn_w[l], w_o[l], hs.dtype)
        f, cbuf_s = conv_ffn(hs, state_conv[l], ffn_norm_w[l], w_up[l], conv_w[l], conv_b[l], w_down[l])
        hs = hs + f
        ret_s.append(S_s.astype(state_ret.dtype))
        ks_.append(kbuf_s.astype(cache_swa_k.dtype))
        vs.append(vbuf_s.astype(cache_swa_v.dtype))
        cs.append(cbuf_s.astype(state_conv.dtype))
    return (hp, hs, jnp.stack(ret_p), jnp.stack(ret_s), jnp.stack(kp), jnp.stack(ks_),
            jnp.stack(vp), jnp.stack(vs), jnp.stack(cp), jnp.stack(cs))
```

```python
import functools

import numpy as np
import jax
import jax.numpy as jnp
from jax import lax
from jax.experimental import pallas as pl
from jax.experimental.pallas import tpu as pltpu

F32 = jnp.float32
BF16 = jnp.bfloat16

D_MODEL = 2048
PAST_LEN = 16384
RET_HEADS = 8
RET_DK = 64
RET_DV = 128
RET_WIDTH = RET_HEADS * RET_DV
RET_CHUNK = 128
SWA_HEADS = 16
SWA_KV_HEADS = 2
SWA_DH = 64
SWA_GROUP = SWA_HEADS // SWA_KV_HEADS
SWA_WIDTH = SWA_HEADS * SWA_DH
WINDOW = 128
SWA_BLOCK = 128
D_FF = 5632
CONV_W = 3
ROPE_THETA = 10000.0
EPS = 1e-6

RQ_W = RET_HEADS * RET_DK
KV_W = SWA_KV_HEADS * SWA_DH
C_RQ, C_RK, C_RV, C_RG = 0, RQ_W, 2 * RQ_W, 2 * RQ_W + RET_WIDTH
C_SQ = C_RG + RET_WIDTH
C_SK = C_SQ + SWA_WIDTH
C_END = C_SK + 2 * KV_W
PROJ_WIDTH = C_END

LANES = 128
SUBLANES = 8
MXU_COLS = 256
VMEM_LIMIT = 56 * 1024 * 1024


def _params(sem):
    return pltpu.CompilerParams(dimension_semantics=sem, vmem_limit_bytes=VMEM_LIMIT)


def _resident(shape):
    nd = len(shape)
    return pl.BlockSpec(shape, lambda *_: (0,) * nd, pipeline_mode=pl.Buffered(1))


def _rope_lanes(x, cos, sin):
    lane = lax.broadcasted_iota(jnp.int32, x.shape, 1)
    first_half = (lane & (SWA_DH - 1)) < (SWA_DH // 2)
    partner = jnp.where(first_half, pltpu.roll(x, LANES - SWA_DH // 2, 1), pltpu.roll(x, SWA_DH // 2, 1))
    return x * cos + partner * sin


def _head_sumsq(y, ones):
    y2 = y * y
    hi = y2.astype(BF16)
    lo = (y2 - hi.astype(F32)).astype(BF16)
    return (jnp.dot(hi, ones, preferred_element_type=F32) + jnp.dot(lo, ones, preferred_element_type=F32))


def _inproj_body(x_ref, nw_ref, w_ref, cos_ref, sin_ref, qn_ref, kn_ref, ones_ref,
                 rq_ref, rk_ref, rv_ref, rg_ref, sq_ref, sk_ref, sv_ref, a_ref):
    x = x_ref[...]
    ms = jnp.mean(x * x, axis=-1, keepdims=True)
    a_ref[...] = (x * lax.rsqrt(ms + EPS) * nw_ref[...]).astype(BF16)
    cos = cos_ref[...]
    sin = sin_ref[...]

    def seg(lo, hi):
        return jnp.dot(a_ref[...], w_ref[:, lo:hi], preferred_element_type=F32)

    rq = seg(C_RQ, C_RK)
    for c in range(RQ_W // LANES):
        sl = slice(c * LANES, (c + 1) * LANES)
        rq_ref[:, sl] = _rope_lanes(rq[:, sl], cos, sin).astype(BF16)
    rk = seg(C_RK, C_RV)
    for c in range(RQ_W // LANES):
        sl = slice(c * LANES, (c + 1) * LANES)
        rk_ref[:, sl] = _rope_lanes(rk[:, sl], cos, sin) * (RET_DK ** -0.5)
    rv_ref[...] = seg(C_RV, C_RG).astype(BF16)
    rg_ref[...] = seg(C_RG, C_SQ)

    sq = seg(C_SQ, C_SK)
    ones = ones_ref[...]
    for c in range(SWA_WIDTH // MXU_COLS):
        y = sq[:, c * MXU_COLS:(c + 1) * MXU_COLS]
        r = lax.rsqrt(_head_sumsq(y, ones) * (1.0 / SWA_DH) + EPS)
        yn = y * r * qn_ref[...]
        for d in range(MXU_COLS // LANES):
            sl = slice(d * LANES, (d + 1) * LANES)
            sq_ref[:, c * MXU_COLS + d * LANES:c * MXU_COLS + (d + 1) * LANES] = (
                _rope_lanes(yn[:, sl], cos, sin) * (SWA_DH ** -0.5)).astype(BF16)
    skv = seg(C_SK, C_END)
    y = skv[:, :KV_W]
    r = lax.rsqrt(_head_sumsq(y, ones[:KV_W, :KV_W]) * (1.0 / SWA_DH) + EPS)
    sk_ref[...] = _rope_lanes(y * r * kn_ref[...], cos, sin)
    sv_ref[...] = skv[:, KV_W:]


def _in_proj(x, norm_w, w_in, cos, sin, qn, kn, ones, tm):
    m = x.shape[0]
    tab_blocks = cos.shape[0] // tm
    row = lambda i: (i, 0)
    outs = [(RQ_W, BF16), (RQ_W, F32), (RET_WIDTH, BF16), (RET_WIDTH, F32), (SWA_WIDTH, BF16), (KV_W, F32), (KV_W, F32)]
    return pl.pallas_call(
        _inproj_body,
        grid=(m // tm,),
        in_specs=[
            pl.BlockSpec((tm, D_MODEL), row),
            _resident((1, D_MODEL)),
            _resident((D_MODEL, PROJ_WIDTH)),
            pl.BlockSpec((tm, LANES), lambda i: (i % tab_blocks, 0)),
            pl.BlockSpec((tm, LANES), lambda i: (i % tab_blocks, 0)),
            _resident((1, MXU_COLS)),
            _resident((1, KV_W)),
            _resident((MXU_COLS, MXU_COLS)),
        ],
        out_specs=[pl.BlockSpec((tm, w), row) for w, _ in outs],
        out_shape=[jax.ShapeDtypeStruct((m, w), dt) for w, dt in outs],
        scratch_shapes=[pltpu.VMEM((tm, D_MODEL), BF16)],
        compiler_params=_params(("arbitrary",)),
        name="in_proj",
    )(x, norm_w, w_in, cos, sin, qn, kn, ones)


def _gn_gate(o, gate, gn_w):
    mu = jnp.mean(o, axis=-1, keepdims=True)
    d = o - mu
    var = jnp.mean(d * d, axis=-1, keepdims=True)
    on = d * lax.rsqrt(var + EPS) * gn_w
    return (jax.nn.silu(gate) * on).astype(BF16)


def _ret_prompt_body(rq_ref, rk_ref, rv_ref, rg_ref, gnw_ref, dmask_ref, rdec_ref, kdec_ref, gc_ref,
                     out_ref, sfin_ref, s_ref, *, n_chunks):
    c = pl.program_id(1)

    @pl.when(c == 0)
    def _():
        s_ref[...] = jnp.zeros_like(s_ref)

    for h in range(RET_HEADS):
        ks = slice(h * RET_DK, (h + 1) * RET_DK)
        vs = slice(h * RET_DV, (h + 1) * RET_DV)
        q = rq_ref[:, ks]
        k32 = rk_ref[:, ks]
        v = rv_ref[:, vs]
        sc = lax.dot_general(q, k32.astype(BF16), (((1,), (1,)), ((), ())), preferred_element_type=F32)
        sc = sc * dmask_ref[h]
        o = jnp.dot(sc.astype(BF16), v, preferred_element_type=F32)
        s_old = s_ref[h]
        o = o + jnp.dot(q, s_old.astype(BF16), preferred_element_type=F32) * rdec_ref[h]
        kd = (k32 * kdec_ref[h]).astype(BF16)
        s_ref[h] = s_old * gc_ref[h] + lax.dot_general(
            kd, v, (((0,), (0,)), ((), ())), preferred_element_type=F32)
        out_ref[:, vs] = _gn_gate(o, rg_ref[:, vs], gnw_ref[:, vs])

    @pl.when(c == n_chunks - 1)
    def _():
        sfin_ref[0] = s_ref[...]


def _decay_tables(chunk):
    lg = jnp.log1p(-jnp.exp2(-5.0 - jnp.arange(RET_HEADS, dtype=F32)))
    idx = jnp.arange(chunk, dtype=F32)
    diff = idx[:, None] - idx[None, :]
    causal = diff >= 0
    dmask = jnp.where(causal[None], jnp.exp(jnp.where(causal, diff, 0.0)[None] * lg[:, None, None]), 0.0)
    rdec = jnp.exp((idx + 1.0)[:, None] * lg[None, :]).T
    kdec = jnp.exp((chunk - 1.0 - idx)[:, None] * lg[None, :]).T
    gc = jnp.exp(chunk * lg)
    return dmask, rdec, kdec, gc


def _ret_prompt(rq, rk, rv, rg, gn_w, batch, seq):
    n_chunks = seq // RET_CHUNK
    dmask, rdec, kdec, gc = _decay_tables(RET_CHUNK)
    rdec = jnp.broadcast_to(rdec[:, :, None], (RET_HEADS, RET_CHUNK, RET_DV))
    kdec = jnp.broadcast_to(kdec[:, :, None], (RET_HEADS, RET_CHUNK, RET_DK))
    gc = jnp.broadcast_to(gc[:, None, None], (RET_HEADS, 1, RET_DV))
    row = lambda b, c: (b * n_chunks + c, 0)
    return pl.pallas_call(
        functools.partial(_ret_prompt_body, n_chunks=n_chunks),
        grid=(batch, n_chunks),
        in_specs=[
            pl.BlockSpec((RET_CHUNK, RQ_W), row),
            pl.BlockSpec((RET_CHUNK, RQ_W), row),
            pl.BlockSpec((RET_CHUNK, RET_WIDTH), row),
            pl.BlockSpec((RET_CHUNK, RET_WIDTH), row),
            _resident((1, RET_WIDTH)),
            _resident((RET_HEADS, RET_CHUNK, RET_CHUNK)),
            _resident((RET_HEADS, RET_CHUNK, RET_DV)),
            _resident((RET_HEADS, RET_CHUNK, RET_DK)),
            _resident((RET_HEADS, 1, RET_DV)),
        ],
        out_specs=[
            pl.BlockSpec((RET_CHUNK, RET_WIDTH), row),
            pl.BlockSpec((1, RET_HEADS, RET_DK, RET_DV), lambda b, c: (b, 0, 0, 0)),
        ],
        out_shape=[
            jax.ShapeDtypeStruct((batch * seq, RET_WIDTH), BF16),
            jax.ShapeDtypeStruct((batch, RET_HEADS, RET_DK, RET_DV), F32),
        ],
        scratch_shapes=[pltpu.VMEM((RET_HEADS, RET_DK, RET_DV), F32)],
        compiler_params=_params(("arbitrary", "arbitrary")),
        name="ret_prompt",
    )(rq, rk, rv, rg, gn_w, dmask, rdec, kdec, gc)


def _ret_sample_body(rq_ref, rk_ref, rv_ref, rg_ref, st_ref, gnw_ref, dmask_ref, rdec_ref, kdec_ref, gc_ref,
                     sel_ref, out_ref, snew_ref, *, bb):
    sel = sel_ref[...] > 0.5
    for h in range(RET_HEADS):
        ks = slice(h * RET_DK, (h + 1) * RET_DK)
        vs = slice(h * RET_DV, (h + 1) * RET_DV)
        q = rq_ref[:, ks]
        k32 = rk_ref[:, ks]
        v = rv_ref[:, vs]
        sc = lax.dot_general(q, k32.astype(BF16), (((1,), (1,)), ((), ())), preferred_element_type=F32)
        dm = dmask_ref[h]
        sc = jnp.where(dm > 0.0, sc * dm, 0.0)
        o = jnp.dot(sc.astype(BF16), v, preferred_element_type=F32)
        s_cat = jnp.concatenate([st_ref[b, h] for b in range(bb)], axis=1)
        oi = jnp.dot(q, s_cat.astype(BF16), preferred_element_type=F32)
        oi = jnp.where(sel, oi, 0.0)
        o_inter = oi[:, :RET_DV]
        for b in range(1, bb):
            o_inter = o_inter + oi[:, b * RET_DV:(b + 1) * RET_DV]
        o = o + o_inter * rdec_ref[h]
        kd = (k32 * kdec_ref[h]).astype(BF16)
        v_big = jnp.where(sel, jnp.concatenate([v] * bb, axis=1), jnp.zeros((), BF16))
        upd = lax.dot_general(kd, v_big, (((0,), (0,)), ((), ())), preferred_element_type=F32)
        s_new = s_cat * jnp.concatenate([gc_ref[h]] * bb, axis=1) + upd
        for b in range(bb):
            snew_ref[b, h] = s_new[:, b * RET_DV:(b + 1) * RET_DV]
        out_ref[:, vs] = _gn_gate(o, rg_ref[:, vs], gnw_ref[:, vs])


def _ret_sample(rq, rk, rv, rg, state, gn_w, n_batch, t, bb):
    rows = bb * t
    dmask, rdec, kdec, gc = _decay_tables(t)
    rb = np.arange(rows) // t
    rt = np.arange(rows) % t
    same = jnp.asarray(rb[:, None] == rb[None, :])
    dmask = jnp.where(same[None], dmask[:, rt][:, :, rt], 0.0)
    rdec = jnp.broadcast_to(rdec[:, rt, None], (RET_HEADS, rows, RET_DV))
    kdec = jnp.broadcast_to(kdec[:, rt, None], (RET_HEADS, rows, RET_DK))
    gc = jnp.broadcast_to(gc[:, None, None], (RET_HEADS, 1, RET_DV))
    sel = jnp.asarray(np.repeat(rb[:, None] == np.arange(bb)[None, :], RET_DV, axis=1), F32)
    row = lambda i: (i, 0)
    st_spec = pl.BlockSpec((bb, RET_HEADS, RET_DK, RET_DV), lambda i: (i, 0, 0, 0))
    return pl.pallas_call(
        functools.partial(_ret_sample_body, bb=bb),
        grid=(n_batch // bb,),
        in_specs=[
            pl.BlockSpec((rows, RQ_W), row),
            pl.BlockSpec((rows, RQ_W), row),
            pl.BlockSpec((rows, RET_WIDTH), row),
            pl.BlockSpec((rows, RET_WIDTH), row),
            st_spec,
            _resident((1, RET_WIDTH)),
            _resident((RET_HEADS, rows, rows)),
            _resident((RET_HEADS, rows, RET_DV)),
            _resident((RET_HEADS, rows, RET_DK)),
            _resident((RET_HEADS, 1, RET_DV)),
            _resident((rows, bb * RET_DV)),
        ],
        out_specs=[pl.BlockSpec((rows, RET_WIDTH), row), st_spec],
        out_shape=[
            jax.ShapeDtypeStruct((n_batch * t, RET_WIDTH), BF16),
            jax.ShapeDtypeStruct((n_batch, RET_HEADS, RET_DK, RET_DV), F32),
        ],
        compiler_params=_params(("arbitrary",)),
        name="ret_sample",
    )(rq, rk, rv, rg, state, gn_w, dmask, rdec, kdec, gc, sel)


def _sink_softmax(s, mask, sink):
    s = jnp.where(mask, s, -jnp.inf)
    m = jnp.maximum(jnp.max(s, axis=-1, keepdims=True), sink)
    p = jnp.exp(s - m)
    return p / (jnp.sum(p, axis=-1, keepdims=True) + jnp.exp(sink - m))


def _swa_prompt_body(sink_ref, q_ref, kp_ref, kc_ref, vp_ref, vc_ref, o_ref):
    n = pl.program_id(1)
    kk = jnp.concatenate([kp_ref[...], kc_ref[...]], axis=0).astype(BF16)
    vv = jnp.concatenate([vp_ref[...], vc_ref[...]], axis=0).astype(BF16)
    qi = lax.broadcasted_iota(jnp.int32, (SWA_BLOCK, 2 * SWA_BLOCK), 0)
    kj = lax.broadcasted_iota(jnp.int32, (SWA_BLOCK, 2 * SWA_BLOCK), 1)
    rel = qi + SWA_BLOCK - kj
    first_key = jnp.where(n > 0, 0, SWA_BLOCK)
    mask = (rel >= 0) & (rel <= WINDOW) & (kj >= first_key)
    for g in range(SWA_KV_HEADS):
        kg = kk[:, g * SWA_DH:(g + 1) * SWA_DH]
        vg = vv[:, g * SWA_DH:(g + 1) * SWA_DH]
        for hh in range(SWA_GROUP):
            h = g * SWA_GROUP + hh
            hs = slice(h * SWA_DH, (h + 1) * SWA_DH)
            s = lax.dot_general(q_ref[:, hs], kg, (((1,), (1,)), ((), ())), preferred_element_type=F32)
            p = _sink_softmax(s, mask, sink_ref[h])
            o_ref[:, hs] = jnp.dot(p.astype(BF16), vg, preferred_element_type=F32).astype(BF16)


def _swa_prompt(sq, sk, sv, sinks, batch, seq):
    nb = seq // SWA_BLOCK
    cur = lambda b, n: (b * nb + n, 0)
    prev = lambda b, n: (b * nb + jnp.maximum(n - 1, 0), 0)
    return pl.pallas_call(
        _swa_prompt_body,
        grid=(batch, nb),
        in_specs=[
            pl.BlockSpec(memory_space=pltpu.SMEM),
            pl.BlockSpec((SWA_BLOCK, SWA_WIDTH), cur),
            pl.BlockSpec((SWA_BLOCK, KV_W), prev),
            pl.BlockSpec((SWA_BLOCK, KV_W), cur),
            pl.BlockSpec((SWA_BLOCK, KV_W), prev),
            pl.BlockSpec((SWA_BLOCK, KV_W), cur),
        ],
        out_specs=pl.BlockSpec((SWA_BLOCK, SWA_WIDTH), cur),
        out_shape=jax.ShapeDtypeStruct((batch * seq, SWA_WIDTH), BF16),
        compiler_params=_params(("arbitrary", "arbitrary")),
        name="swa_prompt",
    )(sinks, sq, sk, sk, sv, sv)


def _swa_sample_body(sink_ref, q_ref, kn_ref, vn_ref, kc_ref, vc_ref, o_ref, ko_ref, vo_ref, *, bb, t):
    rows = t * SWA_GROUP
    n_keys = WINDOW + t
    qt = lax.broadcasted_iota(jnp.int32, (rows, n_keys), 0) // SWA_GROUP
    kj = lax.broadcasted_iota(jnp.int32, (rows, n_keys), 1)
    mask = (kj >= qt) & (kj <= qt + WINDOW)
    for b in range(bb):
        k_all = jnp.concatenate([kc_ref[b], kn_ref[b * t:(b + 1) * t, :]], axis=0)
        v_all = jnp.concatenate([vc_ref[b], vn_ref[b * t:(b + 1) * t, :]], axis=0)
        ko_ref[b] = k_all[t:, :]
        vo_ref[b] = v_all[t:, :]
        kb = k_all.astype(BF16)
        vb = v_all.astype(BF16)
        for g in range(SWA_KV_HEADS):
            gs = slice(g * SWA_DH, (g + 1) * SWA_DH)
            s = lax.dot_general(q_ref[b, g], kb[:, gs], (((1,), (1,)), ((), ())), preferred_element_type=F32)
            p = _sink_softmax(s, mask, sink_ref[g][:, :1])
            o_ref[b, g] = jnp.dot(p.astype(BF16), vb[:, gs], preferred_element_type=F32).astype(BF16)


def _swa_sample(sq, sk, sv, cache_k, cache_v, sinks, n_batch, t, bb):
    rows = t * SWA_GROUP
    q4 = sq.reshape(n_batch, t, SWA_KV_HEADS, SWA_GROUP, SWA_DH).transpose(0, 2, 1, 3, 4)
    q4 = q4.reshape(n_batch, SWA_KV_HEADS, rows, SWA_DH)
    sink_rows = jnp.tile(sinks.astype(F32).reshape(SWA_KV_HEADS, 1, SWA_GROUP), (1, t, 1)).reshape(SWA_KV_HEADS, rows, 1)
    sink_rows = jnp.broadcast_to(sink_rows, (SWA_KV_HEADS, rows, LANES))
    ck = cache_k.reshape(n_batch, WINDOW, KV_W)
    cv = cache_v.reshape(n_batch, WINDOW, KV_W)
    q_spec = pl.BlockSpec((bb, SWA_KV_HEADS, rows, SWA_DH), lambda i: (i, 0, 0, 0))
    new_spec = pl.BlockSpec((bb * t, KV_W), lambda i: (i, 0))
    c_spec = pl.BlockSpec((bb, WINDOW, KV_W), lambda i: (i, 0, 0))
    o4, k_new, v_new = pl.pallas_call(
        functools.partial(_swa_sample_body, bb=bb, t=t),
        grid=(n_batch // bb,),
        in_specs=[_resident((SWA_KV_HEADS, rows, LANES)), q_spec, new_spec, new_spec, c_spec, c_spec],
        out_specs=[q_spec, c_spec, c_spec],
        out_shape=[
            jax.ShapeDtypeStruct((n_batch, SWA_KV_HEADS, rows, SWA_DH), BF16),
            jax.ShapeDtypeStruct((n_batch, WINDOW, KV_W), F32),
            jax.ShapeDtypeStruct((n_batch, WINDOW, KV_W), F32),
        ],
        compiler_params=_params(("arbitrary",)),
        name="swa_sample",
    )(sink_rows, q4, sk, sv, ck, cv)
    o = o4.reshape(n_batch, SWA_KV_HEADS, t, SWA_GROUP, SWA_DH).transpose(0, 2, 1, 3, 4).reshape(n_batch * t, SWA_WIDTH)
    return o, k_new, v_new


def _outproj_body(x_ref, ro_ref, so_ref, wr_ref, ws_ref, nw_ref, h_ref, hn_ref):
    y = jnp.dot(ro_ref[...], wr_ref[...], preferred_element_type=F32)
    y = y + jnp.dot(so_ref[...], ws_ref[...], preferred_element_type=F32)
    h = x_ref[...] + y
    h_ref[...] = h
    ms = jnp.mean(h * h, axis=-1, keepdims=True)
    hn_ref[...] = (h * lax.rsqrt(ms + EPS) * nw_ref[...]).astype(BF16)


def _out_proj(x, ret_o, swa_o, w_o, norm_w, tm):
    m = x.shape[0]
    row = lambda i: (i, 0)
    return pl.pallas_call(
        _outproj_body,
        grid=(m // tm,),
        in_specs=[
            pl.BlockSpec((tm, D_MODEL), row),
            pl.BlockSpec((tm, RET_WIDTH), row),
            pl.BlockSpec((tm, SWA_WIDTH), row),
            pl.BlockSpec((RET_WIDTH, D_MODEL), lambda i: (0, 0), pipeline_mode=pl.Buffered(1)),
            pl.BlockSpec((SWA_WIDTH, D_MODEL), lambda i: (1, 0), pipeline_mode=pl.Buffered(1)),
            _resident((1, D_MODEL)),
        ],
        out_specs=[pl.BlockSpec((tm, D_MODEL), row), pl.BlockSpec((tm, D_MODEL), row)],
        out_shape=[jax.ShapeDtypeStruct((m, D_MODEL), F32), jax.ShapeDtypeStruct((m, D_MODEL), BF16)],
        compiler_params=_params(("arbitrary",)),
        name="out_proj",
    )(x, ret_o, swa_o, w_o, w_o, norm_w)


def _ffn_up_body(*refs, tm, stride, pad, tiles_per_seq, has_prefix):
    if has_prefix:
        hn_ref, wg_ref, wu_ref, cw_ref, cb_ref, pre_ref, act_ref, tail_ref, gp_ref = refs
    else:
        hn_ref, wg_ref, wu_ref, cw_ref, cb_ref, act_ref, tail_ref, gp_ref = refs
    hn = hn_ref[...]
    g = jnp.dot(hn, wg_ref[...], preferred_element_type=F32)
    u = jnp.dot(hn, wu_ref[...], preferred_element_type=F32)
    if has_prefix:
        gp_ref[0:pad, :] = pre_ref[...]
    else:
        @pl.when(pl.program_id(1) % tiles_per_seq == 0)
        def _():
            gp_ref[0:pad, :] = jnp.zeros((pad, g.shape[1]), F32)
    gp_ref[pad:pad + tm, :] = g
    acc = gp_ref[pad - 2 * stride:pad - 2 * stride + tm, :] * cw_ref[0:1, :]
    acc = acc + gp_ref[pad - stride:pad - stride + tm, :] * cw_ref[1:2, :]
    acc = acc + g * cw_ref[2:3, :]
    gc = cb_ref[...] + acc
    act_ref[...] = (jax.nn.silu(gc) * u).astype(BF16)
    tail = gp_ref[tm:tm + pad, :]
    tail_ref[...] = tail.reshape(tail_ref.shape)
    if not has_prefix:
        gp_ref[0:pad, :] = tail


def _ffn_up(hn, w_up, conv_w, conv_b, prefix, *, tm, tf, stride, pad, n_seq):
    m = hn.shape[0]
    nf = D_FF // tf
    has_prefix = prefix is not None
    tiles_per_seq = (m // n_seq) // tm
    in_specs = [
        pl.BlockSpec((tm, D_MODEL), lambda j, i: (i, 0)),
        pl.BlockSpec((D_MODEL, tf), lambda j, i: (0, j)),
        pl.BlockSpec((D_MODEL, tf), lambda j, i: (0, nf + j)),
        pl.BlockSpec((CONV_W, tf), lambda j, i: (0, j)),
        pl.BlockSpec((1, tf), lambda j, i: (0, j)),
    ]
    args = [hn, w_up, w_up, conv_w, conv_b]
    if has_prefix:
        in_specs.append(pl.BlockSpec((pad, tf), lambda j, i: (0, j)))
        args.append(prefix)
    return pl.pallas_call(
        functools.partial(_ffn_up_body, tm=tm, stride=stride, pad=pad, tiles_per_seq=tiles_per_seq,
                          has_prefix=has_prefix),
        grid=(nf, m // tm),
        in_specs=in_specs,
        out_specs=[
            pl.BlockSpec((tm, tf), lambda j, i: (i, j)),
            pl.BlockSpec((1, pad, tf), lambda j, i: (i // tiles_per_seq, 0, j)),
        ],
        out_shape=[jax.ShapeDtypeStruct((m, D_FF), BF16), jax.ShapeDtypeStruct((n_seq, pad, D_FF), F32)],
        scratch_shapes=[pltpu.VMEM((pad + tm, tf), F32)],
        compiler_params=_params(("arbitrary", "arbitrary")),
        name="ffn_up",
    )(*args)


def _ffn_down_body(h_ref, act_ref, w_ref, o_ref):
    o_ref[...] = h_ref[...] + jnp.dot(act_ref[...], w_ref[...], preferred_element_type=F32)


def _ffn_down(h, act, w_down, tm, tn):
    m = h.shape[0]
    return pl.pallas_call(
        _ffn_down_body,
        grid=(D_MODEL // tn, m // tm),
        in_specs=[
            pl.BlockSpec((tm, tn), lambda j, i: (i, j)),
            pl.BlockSpec((tm, D_FF), lambda j, i: (i, 0)),
            pl.BlockSpec((D_FF, tn), lambda j, i: (0, j)),
        ],
        out_specs=pl.BlockSpec((tm, tn), lambda j, i: (i, j)),
        out_shape=jax.ShapeDtypeStruct((m, D_MODEL), F32),
        compiler_params=_params(("arbitrary", "arbitrary")),
        name="ffn_down",
    )(h, act, w_down)


def _rope_tables(pos):
    half = SWA_DH // 2
    inv = ROPE_THETA ** (-jnp.arange(half, dtype=F32) / half)
    ang = pos.astype(F32)[:, None] * inv[None, :]
    cos = jnp.cos(ang)
    sin = jnp.sin(ang)
    reps = LANES // SWA_DH
    return (jnp.tile(jnp.concatenate([cos, cos], axis=1), (1, reps)),
            jnp.tile(jnp.concatenate([-sin, sin], axis=1), (1, reps)))


def _head_ones():
    hid = np.arange(MXU_COLS) // SWA_DH
    return jnp.asarray(hid[:, None] == hid[None, :], BF16)


def _layer(x, pos_tables, weights, attn_fn, tm):
    (attn_norm_w, w_in, qn, kn, ones, ret_gn_w, w_o, ffn_norm_w) = weights
    cos, sin = pos_tables
    rq, rk, rv, rg, sq, sk, sv = _in_proj(x, attn_norm_w, w_in, cos, sin, qn, kn, ones, tm)
    ret_o, swa_o, extras = attn_fn(rq, rk, rv, rg, sq, sk, sv)
    h, hn = _out_proj(x, ret_o, swa_o, w_o, ffn_norm_w, tm)
    return h, hn, sk, sv, extras


def kernel(x_prompt, x_sample, state_ret, cache_swa_k, cache_swa_v, state_conv, attn_norm_w, w_in, swa_q_norm_w,
           swa_k_norm_w, swa_sinks, ret_gn_w, w_o, ffn_norm_w, w_up, conv_w, conv_b, w_down):
    batch, seq, _ = x_prompt.shape
    n_dec, t_dec, _ = x_sample.shape
    assert state_ret.shape[0] == 1, "single layer"
    l = 0
    w_in_b = w_in[l].astype(BF16)
    w_o_b = w_o[l].astype(BF16)
    w_up_b = w_up[l].astype(BF16)
    w_down_b = w_down[l].astype(BF16)
    reps_q = MXU_COLS // SWA_DH
    weights = (attn_norm_w[l][None, :], w_in_b, jnp.tile(swa_q_norm_w[l], reps_q)[None, :],
               jnp.tile(swa_k_norm_w[l], SWA_KV_HEADS)[None, :], _head_ones(), ret_gn_w[l][None, :], w_o_b,
               ffn_norm_w[l][None, :])
    sinks = swa_sinks[l].astype(F32)
    cw = conv_w[l]
    cb = conv_b[l][None, :]

    tm = 512
    xp = x_prompt.reshape(batch * seq, D_MODEL)

    def attn_prompt(rq, rk, rv, rg, sq, sk, sv):
        ret_o, s_fin = _ret_prompt(rq, rk, rv, rg, weights[5], batch, seq)
        swa_o = _swa_prompt(sq, sk, sv, sinks, batch, seq)
        return ret_o, swa_o, s_fin

    hp, hnp, skp, svp, s_p = _layer(xp, _rope_tables(jnp.arange(seq, dtype=jnp.int32)), weights, attn_prompt, tm)
    act_p, tail_p = _ffn_up(hnp, w_up_b, cw, cb, None, tm=1024, tf=512, stride=1, pad=SUBLANES, n_seq=batch)
    y_p = _ffn_down(hp, act_p, w_down_b, tm, 1024).reshape(batch, seq, D_MODEL)
    k_p = skp.reshape(batch, seq, SWA_KV_HEADS, SWA_DH)[:, -WINDOW:]
    v_p = svp.reshape(batch, seq, SWA_KV_HEADS, SWA_DH)[:, -WINDOW:]
    conv_p = tail_p[:, -(CONV_W - 1):, :]

    m_s = n_dec * t_dec
    xs = x_sample.reshape(m_s, D_MODEL)
    bb = 8

    def attn_sample(rq, rk, rv, rg, sq, sk, sv):
        ret_o, s_new = _ret_sample(rq, rk, rv, rg, state_ret[l], weights[5], n_dec, t_dec, bb)
        swa_o, k_new, v_new = _swa_sample(sq, sk, sv, cache_swa_k[l], cache_swa_v[l], sinks, n_dec, t_dec, bb)
        return ret_o, swa_o, (s_new, k_new, v_new)

    pos_s = PAST_LEN + (jnp.arange(m_s, dtype=jnp.int32) % t_dec)
    hs, hns, _, _, (s_s, k_s, v_s) = _layer(xs, _rope_tables(pos_s), weights, attn_sample, m_s)
    hns_t = hns.reshape(n_dec, t_dec, D_MODEL).transpose(1, 0, 2).reshape(m_s, D_MODEL)
    prefix = state_conv[l].transpose(1, 0, 2).reshape((CONV_W - 1) * n_dec, D_FF)
    act_t, tail_s = _ffn_up(hns_t, w_up_b, cw, cb, prefix, tm=m_s, tf=512, stride=n_dec, pad=(CONV_W - 1) * n_dec,
                            n_seq=1)
    act_s = act_t.reshape(t_dec, n_dec, D_FF).transpose(1, 0, 2).reshape(m_s, D_FF)
    y_s = _ffn_down(hs, act_s, w_down_b, m_s, 1024).reshape(n_dec, t_dec, D_MODEL)
    conv_s = tail_s.reshape(CONV_W - 1, n_dec, D_FF).transpose(1, 0, 2)

    kv_shape = (n_dec, WINDOW, SWA_KV_HEADS, SWA_DH)
    return (y_p, y_s, s_p[None], s_s[None], k_p[None], k_s.reshape(kv_shape)[None], v_p[None],
            v_s.reshape(kv_shape)[None], conv_p[None], conv_s[None])
```
